```python
import math
import jax
import jax.numpy as jnp
from jax import lax
import numpy as np

D_MODEL = 1024
BATCH = 4
SEQ = 8192
DEPTH = 2

N_EVEN = (DEPTH + 1) // 2
N_ODD = DEPTH // 2
HEAD_DIM = 64
NORM_EPS = 1e-6

LRU_WIDTH = D_MODEL // 2
LRU_BLOCKS = LRU_WIDTH // HEAD_DIM
LRU_BLOCK = LRU_WIDTH // LRU_BLOCKS
CONV_WIDTH = 4
LRU_C = 8.0
MOBA_HEADS = (D_MODEL // 2) // HEAD_DIM
MOBA_WIDTH = MOBA_HEADS * HEAD_DIM
MOBA_BLOCK = 256
MOBA_TOPK = 3
MOBA_QCHUNK = 32
EVEN_SPLITS = (LRU_WIDTH, LRU_WIDTH, MOBA_WIDTH, MOBA_WIDTH, MOBA_WIDTH, MOBA_WIDTH)
EVEN_IN = sum(EVEN_SPLITS)
EVEN_MIX = LRU_WIDTH + MOBA_WIDTH

SB_HEADS = (D_MODEL // 2) // HEAD_DIM
SB_WIDTH = SB_HEADS * HEAD_DIM
SB_QBLOCK = 128
S5_WIDTH = D_MODEL // 2
S5_GROUP = 16
S5_GROUPS = S5_WIDTH // S5_GROUP
S5_STATE = 64
ODD_SPLITS = (SB_WIDTH, SB_WIDTH, SB_WIDTH, SB_WIDTH, S5_WIDTH, S5_WIDTH)
ODD_IN = sum(ODD_SPLITS)
ODD_MIX = SB_WIDTH + S5_WIDTH

kernel_name = 'hybrid_rglru_moba_stickbreak_s5'


def split_points(sizes):
    return [int(s) for s in np.cumsum(sizes)[:-1]]


def rms_norm(x, gain):
    xf = x.astype(jnp.float32)
    y = xf * lax.rsqrt(jnp.mean(xf * xf, axis=-1, keepdims=True) + NORM_EPS)
    return (y * gain.astype(jnp.float32)).astype(x.dtype)


def ada_modulation(c, w, b):
    mod = jnp.einsum('bd,de->be', jax.nn.silu(c), w) + b
    shift, scale, gate = jnp.split(mod[:, None, :], 3, axis=-1)
    return shift, scale, gate


def to_heads(t, n_heads):
    b_, t_, _ = t.shape
    return t.reshape(b_, t_, n_heads, HEAD_DIM)


def from_heads(t):
    b_, h_, t_, d_ = t.shape
    return t.transpose(0, 2, 1, 3).reshape(b_, t_, h_ * d_)


def causal_depthwise_conv(x, w, b):
    ch = x.shape[-1]
    y = lax.conv_general_dilated(x, w[:, None, :].astype(x.dtype), window_strides=(1,),
                                 padding=[(CONV_WIDTH - 1, 0)],
                                 dimension_numbers=('NWC', 'WIO', 'NWC'),
                                 feature_group_count=ch)
    return y + b


def rg_lru(x, rg_w, rg_b, ig_w, ig_b, lam):
    b_, t_, w_ = x.shape
    xf = x.astype(jnp.float32)
    xg = xf.reshape(b_, t_, LRU_BLOCKS, LRU_BLOCK)
    r = jax.nn.sigmoid(jnp.einsum('btgi,gij->btgj', xg, rg_w.astype(jnp.float32)) + rg_b).reshape(b_, t_, w_)
    i = jax.nn.sigmoid(jnp.einsum('btgi,gij->btgj', xg, ig_w.astype(jnp.float32)) + ig_b).reshape(b_, t_, w_)
    log_a = -LRU_C * r * jax.nn.softplus(-lam.astype(jnp.float32))
    a = jnp.exp(log_a)
    inp = jnp.sqrt(-jnp.expm1(2.0 * log_a)) * (i * xf)

    def combine(left, right):
        a1, b1 = left
        a2, b2 = right
        return a1 * a2, a2 * b1 + b2

    _, h = lax.associative_scan(combine, (a, inp), axis=1)
    return h


def moba_attention(q, k, v):
    b_, h_, t_, d_ = q.shape
    nb = -(-t_ // MOBA_BLOCK)
    pad = nb * MOBA_BLOCK - t_
    kp = jnp.pad(k, ((0, 0), (0, 0), (0, pad), (0, 0))).reshape(b_, h_, nb, MOBA_BLOCK, d_)
    vp = jnp.pad(v, ((0, 0), (0, 0), (0, pad), (0, 0))).reshape(b_, h_, nb, MOBA_BLOCK, d_)
    k_mean = jnp.mean(kp, axis=3)
    q_blk = jnp.arange(t_) // MOBA_BLOCK
    gate = jnp.einsum('bhtd,bhnd->bhtn', q, k_mean)
    past = jnp.arange(nb)[None, :] < q_blk[:, None]
    gate = jnp.where(past, gate, -jnp.inf)
    topk = min(MOBA_TOPK, nb)
    _, sel = lax.top_k(gate, topk)
    sel_valid = sel < q_blk[:, None]
    nc = t_ // MOBA_QCHUNK
    scale = d_ ** -0.5
    gather = jax.vmap(jax.vmap(lambda blocks, idx: blocks[idx]))

    def chunk(a):
        return jnp.moveaxis(a.reshape((b_, h_, nc, MOBA_QCHUNK) + a.shape[3:]), 2, 0)

    def attend(args):
        ci, qc, selc, validc = args
        q0 = ci * MOBA_QCHUNK
        qpos = q0 + jnp.arange(MOBA_QCHUNK)
        own = q0 // MOBA_BLOCK
        ks = gather(kp, selc)
        vs = gather(vp, selc)
        s_sel = jnp.einsum('bhqd,bhqnkd->bhqnk', qc, ks) * scale
        s_sel = jnp.where(validc[..., None], s_sel, -jnp.inf).reshape(b_, h_, MOBA_QCHUNK, topk * MOBA_BLOCK)
        k_own = lax.dynamic_index_in_dim(kp, own, axis=2, keepdims=False)
        v_own = lax.dynamic_index_in_dim(vp, own, axis=2, keepdims=False)
        s_own = jnp.einsum('bhqd,bhkd->bhqk', qc, k_own) * scale
        kpos = own * MOBA_BLOCK + jnp.arange(MOBA_BLOCK)
        s_own = jnp.where(kpos[None, :] <= qpos[:, None], s_own, -jnp.inf)
        p = jax.nn.softmax(jnp.concatenate([s_sel, s_own], axis=-1), axis=-1)
        p_sel = p[..., :topk * MOBA_BLOCK].reshape(b_, h_, MOBA_QCHUNK, topk, MOBA_BLOCK)
        p_own = p[..., topk * MOBA_BLOCK:]
        return (jnp.einsum('bhqnk,bhqnkd->bhqd', p_sel, vs)
                + jnp.einsum('bhqk,bhkd->bhqd', p_own, v_own))

    out = lax.map(attend, (jnp.arange(nc), chunk(q), chunk(sel), chunk(sel_valid)))
    return jnp.moveaxis(out, 0, 2).reshape(b_, h_, t_, d_)


def stick_breaking_attention(q, k, v):
    t_ = q.shape[2]
    scale = q.shape[-1] ** -0.5
    outs = []
    for blk in range(t_ // SB_QBLOCK):
        q0 = blk * SB_QBLOCK
        end = q0 + SB_QBLOCK
        z = jnp.einsum('bhqd,bhkd->bhqk', q[:, :, q0:end], k[:, :, :end]) * scale
        qpos = q0 + jnp.arange(SB_QBLOCK)
        strict = jnp.arange(end)[None, :] < qpos[:, None]
        log_beta = jax.nn.log_sigmoid(z)
        log_keep = jnp.where(strict, jax.nn.log_sigmoid(-z), 0.0)
        later = lax.cumsum(log_keep, axis=3, reverse=True) - log_keep
        w = jnp.where(strict, jnp.exp(log_beta + later), 0.0)
        outs.append(jnp.einsum('bhqk,bhkd->bhqd', w, v[:, :, :end]))
    return jnp.concatenate(outs, axis=2)


def s5_ssm(u, lam_re, lam_im, log_step, b_re, b_im, c_re, c_im, d):
    f32 = jnp.float32
    lam_re = lam_re.astype(f32)
    lam_im = lam_im.astype(f32)
    step = jnp.exp(log_step.astype(f32))[:, None]
    decay = jnp.exp(lam_re * step)
    ab_re = decay * jnp.cos(lam_im * step)
    ab_im = decay * jnp.sin(lam_im * step)
    den = lam_re * lam_re + lam_im * lam_im
    f_re = ((ab_re - 1.0) * lam_re + ab_im * lam_im) / den
    f_im = (ab_im * lam_re - (ab_re - 1.0) * lam_im) / den
    b_re = b_re.astype(f32)
    b_im = b_im.astype(f32)
    bb_re = f_re[..., None] * b_re - f_im[..., None] * b_im
    bb_im = f_re[..., None] * b_im + f_im[..., None] * b_re
    bu_re = jnp.einsum('btgh,gph->btgp', u, bb_re)
    bu_im = jnp.einsum('btgh,gph->btgp', u, bb_im)
    t_ = u.shape[1]
    a_re = jnp.broadcast_to(ab_re[None, None], (1, t_) + ab_re.shape)
    a_im = jnp.broadcast_to(ab_im[None, None], (1, t_) + ab_im.shape)

    def combine(left, right):
        ar1, ai1, br1, bi1 = left
        ar2, ai2, br2, bi2 = right
        return (ar2 * ar1 - ai2 * ai1, ar2 * ai1 + ai2 * ar1,
                ar2 * br1 - ai2 * bi1 + br2, ar2 * bi1 + ai2 * br1 + bi2)

    _, _, x_re, x_im = lax.associative_scan(combine, (a_re, a_im, bu_re, bu_im), axis=1)
    y = (jnp.einsum('btgp,ghp->btgh', x_re, c_re.astype(f32))
         - jnp.einsum('btgp,ghp->btgh', x_im, c_im.astype(f32))
         + d.astype(f32) * u)
    return y


def even_layer(x, c, norm_g, ada_w, ada_b, w_in, conv_w, conv_b, rg_w, rg_b, ig_w, ig_b,
               lam, q_g, k_g, w_out):
    shift, scale, gate = ada_modulation(c, ada_w, ada_b)
    h = rms_norm(x, norm_g) * (1.0 + scale) + shift
    proj = jnp.einsum('btd,de->bte', h, w_in)
    x_lru, g_lru, q, k, v, g_att = jnp.split(proj, split_points(EVEN_SPLITS), axis=-1)
    xc = causal_depthwise_conv(x_lru, conv_w, conv_b)
    y_lru = rg_lru(xc, rg_w, rg_b, ig_w, ig_b, lam) * jax.nn.silu(g_lru.astype(jnp.float32))
    qh = rms_norm(to_heads(q, MOBA_HEADS), q_g).astype(jnp.float32).transpose(0, 2, 1, 3)
    kh = rms_norm(to_heads(k, MOBA_HEADS), k_g).astype(jnp.float32).transpose(0, 2, 1, 3)
    vh = to_heads(v, MOBA_HEADS).astype(jnp.float32).transpose(0, 2, 1, 3)
    y_att = from_heads(moba_attention(qh, kh, vh)) * jax.nn.silu(g_att.astype(jnp.float32))
    mix = jnp.concatenate([y_lru, y_att], axis=-1).astype(x.dtype)
    return x + gate * jnp.einsum('bte,ed->btd', mix, w_out)


def odd_layer(x, c, norm_g, ada_w, ada_b, w_in, q_g, k_g, lam_re, lam_im, log_step,
              b_re, b_im, c_re, c_im, d, glu_w, glu_b, w_out):
    b_, t_, _ = x.shape
    shift, scale, gate = ada_modulation(c, ada_w, ada_b)
    h = rms_norm(x, norm_g) * (1.0 + scale) + shift
    proj = jnp.einsum('btd,de->bte', h, w_in)
    q, k, v, g_sb, u, g_s5 = jnp.split(proj, split_points(ODD_SPLITS), axis=-1)
    qh = rms_norm(to_heads(q, SB_HEADS), q_g).astype(jnp.float32).transpose(0, 2, 1, 3)
    kh = rms_norm(to_heads(k, SB_HEADS), k_g).astype(jnp.float32).transpose(0, 2, 1, 3)
    vh = to_heads(v, SB_HEADS).astype(jnp.float32).transpose(0, 2, 1, 3)
    y_sb = from_heads(stick_breaking_attention(qh, kh, vh)) * jax.nn.silu(g_sb.astype(jnp.float32))
    ug = u.astype(jnp.float32).reshape(b_, t_, S5_GROUPS, S5_GROUP)
    y = s5_ssm(ug, lam_re, lam_im, log_step, b_re, b_im, c_re, c_im, d).reshape(b_, t_, S5_WIDTH)
    z = jnp.einsum('bte,ef->btf', y, glu_w.astype(jnp.float32)) + glu_b.astype(jnp.float32)
    z_val, z_gate = jnp.split(z, 2, axis=-1)
    y_s5 = z_val * jax.nn.sigmoid(z_gate) * jax.nn.silu(g_s5.astype(jnp.float32))
    mix = jnp.concatenate([y_sb, y_s5], axis=-1).astype(x.dtype)
    return x + gate * jnp.einsum('bte,ed->btd', mix, w_out)


def setup_inputs(seed: int = 0) -> dict:
    key = jax.random.key(seed)
    keys = iter(jax.random.split(key, 40))
    f32 = jnp.float32

    def nrm(shape, s):
        return jax.random.normal(next(keys), shape, f32) * s

    def gain(shape):
        return 1.0 + nrm(shape, 0.02)

    ne, no = N_EVEN, N_ODD
    x = nrm((BATCH, SEQ, D_MODEL), 1.0)
    c = nrm((BATCH, D_MODEL), 1.0)
    u_lru = jax.random.uniform(next(keys), (ne, LRU_WIDTH), f32, minval=0.9, maxval=0.999)
    a_lru = u_lru ** (1.0 / LRU_C)
    lam_im = (jnp.pi * jnp.arange(S5_STATE, dtype=f32))[None, None, :] + nrm((no, S5_GROUPS, S5_STATE), 0.01)
    return {
        'x': x,
        'c': c,
        'ev_norm': gain((ne, D_MODEL)),
        'ev_ada_w': nrm((ne, D_MODEL, 3 * D_MODEL), 0.5 * D_MODEL ** -0.5),
        'ev_ada_b': nrm((ne, 3 * D_MODEL), 0.02),
        'ev_w_in': nrm((ne, D_MODEL, EVEN_IN), D_MODEL ** -0.5),
        'ev_conv_w': nrm((ne, CONV_WIDTH, LRU_WIDTH), CONV_WIDTH ** -0.5),
        'ev_conv_b': nrm((ne, LRU_WIDTH), 0.02),
        'ev_rgate_w': nrm((ne, LRU_BLOCKS, LRU_BLOCK, LRU_BLOCK), LRU_BLOCK ** -0.5),
        'ev_rgate_b': nrm((ne, LRU_BLOCKS, LRU_BLOCK), 0.02),
        'ev_igate_w': nrm((ne, LRU_BLOCKS, LRU_BLOCK, LRU_BLOCK), LRU_BLOCK ** -0.5),
        'ev_igate_b': nrm((ne, LRU_BLOCKS, LRU_BLOCK), 0.02),
        'ev_lru_lambda': jnp.log(a_lru) - jnp.log1p(-a_lru),
        'ev_q_norm': gain((ne, HEAD_DIM)),
        'ev_k_norm': gain((ne, HEAD_DIM)),
        'ev_w_out': nrm((ne, EVEN_MIX, D_MODEL), EVEN_MIX ** -0.5),
        'od_norm': gain((no, D_MODEL)),
        'od_ada_w': nrm((no, D_MODEL, 3 * D_MODEL), 0.5 * D_MODEL ** -0.5),
        'od_ada_b': nrm((no, 3 * D_MODEL), 0.02),
        'od_w_in': nrm((no, D_MODEL, ODD_IN), D_MODEL ** -0.5),
        'od_q_norm': gain((no, HEAD_DIM)),
        'od_k_norm': gain((no, HEAD_DIM)),
        'od_s5_lambda_re': -0.5 + nrm((no, S5_GROUPS, S5_STATE), 0.01),
        'od_s5_lambda_im': lam_im,
        'od_s5_log_step': jax.random.uniform(next(keys), (no, S5_GROUPS), f32,
                                             minval=math.log(1e-3), maxval=math.log(1e-1)),
        'od_s5_b_re': nrm((no, S5_GROUPS, S5_STATE, S5_GROUP), (2 * S5_GROUP) ** -0.5),
        'od_s5_b_im': nrm((no, S5_GROUPS, S5_STATE, S5_GROUP), (2 * S5_GROUP) ** -0.5),
        'od_s5_c_re': nrm((no, S5_GROUPS, S5_GROUP, S5_STATE), S5_STATE ** -0.5),
        'od_s5_c_im': nrm((no, S5_GROUPS, S5_GROUP, S5_STATE), S5_STATE ** -0.5),
        'od_s5_d': nrm((no, S5_GROUPS, S5_GROUP), 1.0),
        'od_glu_w': nrm((no, S5_WIDTH, 2 * S5_WIDTH), S5_WIDTH ** -0.5),
        'od_glu_b': nrm((no, 2 * S5_WIDTH), 0.02),
        'od_w_out': nrm((no, ODD_MIX, D_MODEL), ODD_MIX ** -0.5),
    }


def reference(x, c, ev_norm, ev_ada_w, ev_ada_b, ev_w_in, ev_conv_w, ev_conv_b, ev_rgate_w,
              ev_rgate_b, ev_igate_w, ev_igate_b, ev_lru_lambda, ev_q_norm, ev_k_norm, ev_w_out,
              od_norm, od_ada_w, od_ada_b, od_w_in, od_q_norm, od_k_norm, od_s5_lambda_re,
              od_s5_lambda_im, od_s5_log_step, od_s5_b_re, od_s5_b_im, od_s5_c_re, od_s5_c_im,
              od_s5_d, od_glu_w, od_glu_b, od_w_out):
    for layer in range(DEPTH):
        i = layer // 2
        if layer % 2 == 0:
            x = even_layer(x, c, ev_norm[i], ev_ada_w[i], ev_ada_b[i], ev_w_in[i], ev_conv_w[i],
                           ev_conv_b[i], ev_rgate_w[i], ev_rgate_b[i], ev_igate_w[i], ev_igate_b[i],
                           ev_lru_lambda[i], ev_q_norm[i], ev_k_norm[i], ev_w_out[i])
        else:
            x = odd_layer(x, c, od_norm[i], od_ada_w[i], od_ada_b[i], od_w_in[i], od_q_norm[i],
                          od_k_norm[i], od_s5_lambda_re[i], od_s5_lambda_im[i], od_s5_log_step[i],
                          od_s5_b_re[i], od_s5_b_im[i], od_s5_c_re[i], od_s5_c_im[i], od_s5_d[i],
                          od_glu_w[i], od_glu_b[i], od_w_out[i])
    return x
```

```python
import functools
import math

import jax
import jax.numpy as jnp
from jax import lax
from jax.experimental import pallas as pl
from jax.experimental.pallas import tpu as pltpu

F32 = jnp.float32
BF16 = jnp.bfloat16

HEAD_DIM = 64
NORM_EPS = 1e-6
LRU_C = 8.0
CONV_WIDTH = 4
MOBA_BLOCK = 256
MOBA_TOPK = 3
S5_GROUP = 16
S5_STATE = 64
S5_CHUNK = 16
SEG = 512
LANES = 128
ATT_BLOCK = 256
MASK_BIAS = -(2.0 ** 100)
SB_DEAD = -104.0
VMEM_LIMIT = 56 * 1024 * 1024

_NT = (((1,), (1,)), ((), ()))


def _sigmoid(x):
    return 1.0 / (1.0 + jnp.exp(-x))


def _silu(x):
    return x * _sigmoid(x)


def _softplus(x):
    return jnp.maximum(x, 0.0) + jnp.log1p(jnp.exp(-jnp.abs(x)))


def _params(sem):
    return pltpu.CompilerParams(dimension_semantics=sem, vmem_limit_bytes=VMEM_LIMIT)


def _mod_kernel(c_ref, w_ref, b_ref, o_ref):
    sc = _silu(c_ref[...])
    o_ref[...] = jnp.dot(sc, w_ref[...], preferred_element_type=F32,
                         precision=lax.Precision.HIGHEST) + b_ref[...]


def _modulation(c, w, b):
    bsz, d = c.shape
    n = w.shape[1]
    tn = 1024
    out = pl.pallas_call(
        _mod_kernel,
        grid=(n // tn,),
        in_specs=[pl.BlockSpec((bsz, d), lambda j: (0, 0)),
                  pl.BlockSpec((d, tn), lambda j: (0, j)),
                  pl.BlockSpec((1, tn), lambda j: (0, j))],
        out_specs=pl.BlockSpec((bsz, tn), lambda j: (0, j)),
        out_shape=jax.ShapeDtypeStruct((bsz, n), F32),
        compiler_params=_params(("arbitrary",)),
        name="modulation",
    )(c, w, b.reshape(1, n))
    return out.reshape(bsz, 3, d)


def _inproj_kernel(kinds, x_ref, mod_ref, ng_ref, w_ref, qg_ref, kg_ref, ones_ref, *out_refs):
    x = x_ref[0]
    ms = jnp.mean(x * x, axis=-1, keepdims=True)
    y = x * lax.rsqrt(ms + NORM_EPS) * ng_ref[...]
    m = mod_ref[0]
    h = (y * (1.0 + m[1:2]) + m[0:1]).astype(BF16)
    for s, kind in enumerate(kinds):
        p = jnp.dot(h, w_ref[:, s * SEG:(s + 1) * SEG], preferred_element_type=F32)
        if kind in ("q", "k"):
            ssq = jnp.dot((p * p).astype(BF16), ones_ref[...], preferred_element_type=F32)
            gain = qg_ref if kind == "q" else kg_ref
            p = p * lax.rsqrt(ssq * (1.0 / HEAD_DIM) + NORM_EPS) * gain[...]
        out_refs[s][0] = p.astype(out_refs[s].dtype)


def _in_projection(x, mod, norm_g, w_in, q_gain, k_gain, kinds, dtypes, tm=512):
    bsz, t, d = x.shape
    nseg = len(kinds)
    head_ones = jnp.kron(jnp.eye(SEG // HEAD_DIM, dtype=F32),
                         jnp.ones((HEAD_DIM, HEAD_DIM), F32)).astype(BF16)
    row = lambda b, i: (b, i, 0)
    const = lambda b, i: (0, 0)
    return pl.pallas_call(
        functools.partial(_inproj_kernel, kinds),
        grid=(bsz, t // tm),
        in_specs=[pl.BlockSpec((1, tm, d), row),
                  pl.BlockSpec((1, 3, d), lambda b, i: (b, 0, 0)),
                  pl.BlockSpec((1, d), const),
                  pl.BlockSpec((d, nseg * SEG), const),
                  pl.BlockSpec((1, SEG), const),
                  pl.BlockSpec((1, SEG), const),
                  pl.BlockSpec((SEG, SEG), const)],
        out_specs=[pl.BlockSpec((1, tm, SEG), row) for _ in kinds],
        out_shape=[jax.ShapeDtypeStruct((bsz, t, SEG), dt) for dt in dtypes],
        compiler_params=_params(("arbitrary", "arbitrary")),
        name="in_projection",
    )(x, mod, norm_g.reshape(1, d), w_in.astype(BF16), q_gain, k_gain, head_ones)


def _lru_kernel(x_ref, g_ref, cw_ref, cb_ref, wg_ref, bg_ref, lam_ref, o_ref,
                xbuf, a_s, b_s, hcar):
    tt = x_ref.shape[1]
    w = x_ref.shape[2]
    ng = tt // 8

    @pl.when(pl.program_id(1) == 0)
    def _():
        xbuf[0:8, :] = jnp.zeros((8, w), F32)
        hcar[...] = jnp.zeros((8, w), F32)

    x = x_ref[0]
    xbuf[8:8 + tt, :] = x
    cw = cw_ref[...]
    xc = cb_ref[...] + cw[CONV_WIDTH - 1:CONV_WIDTH] * x
    for j in range(CONV_WIDTH - 1):
        xc = xc + cw[j:j + 1] * xbuf[5 + j:5 + j + tt, :]
    xbuf[0:8, :] = x[tt - 8:tt, :]

    gates = jnp.dot(xc.astype(BF16), wg_ref[...], preferred_element_type=F32) + bg_ref[...]
    r = _sigmoid(gates[:, :w])
    i = _sigmoid(gates[:, w:])
    log_a = (-LRU_C) * r * _softplus(-lam_ref[...])
    a = jnp.exp(log_a)
    b = jnp.sqrt(-jnp.tanh(log_a) * (a * a + 1.0)) * (i * xc)

    sub = lax.broadcasted_iota(jnp.int32, (tt, w), 0) % 8
    for dist in (1, 2, 4):
        a_prev = pltpu.roll(a, dist, axis=0)
        b_prev = pltpu.roll(b, dist, axis=0)
        keep = sub >= dist
        b = jnp.where(keep, a * b_prev + b, b)
        a = jnp.where(keep, a * a_prev, a)
    a_s[...] = a
    b_s[...] = b

    def body(gi, hprev):
        rows = pl.ds(pl.multiple_of(gi * 8, 8), 8)
        h = a_s[rows, :] * hprev + b_s[rows, :]
        b_s[rows, :] = h
        return h[7:8, :]

    hcar[0:1, :] = lax.fori_loop(0, ng, body, hcar[0:1, :])
    gate = g_ref[0]
    o_ref[0] = (b_s[...] * _silu(gate)).astype(o_ref.dtype)


def _block_diag(blocks):
    n, k, _ = blocks.shape
    eye = jnp.eye(n, dtype=blocks.dtype)
    return jnp.einsum("gij,gh->gihj", blocks, eye).reshape(n * k, n * k)


def _rg_lru(x_lru, g_lru, conv_w, conv_b, rg_w, rg_b, ig_w, ig_b, lam, tt=512):
    bsz, t, w = x_lru.shape
    wg = jnp.concatenate([_block_diag(rg_w), _block_diag(ig_w)], axis=1).astype(BF16)
    bg = jnp.concatenate([rg_b.reshape(1, w), ig_b.reshape(1, w)], axis=1)
    row = lambda b, i: (b, i, 0)
    const = lambda b, i: (0, 0)
    return pl.pallas_call(
        _lru_kernel,
        grid=(bsz, t // tt),
        in_specs=[pl.BlockSpec((1, tt, w), row),
                  pl.BlockSpec((1, tt, w), row),
                  pl.BlockSpec((CONV_WIDTH, w), const),
                  pl.BlockSpec((1, w), const),
                  pl.BlockSpec((w, 2 * w), const),
                  pl.BlockSpec((1, 2 * w), const),
                  pl.BlockSpec((1, w), const)],
        out_specs=pl.BlockSpec((1, tt, w), row),
        out_shape=jax.ShapeDtypeStruct((bsz, t, w), BF16),
        scratch_shapes=[pltpu.VMEM((tt + 8, w), F32),
                        pltpu.VMEM((tt, w), F32),
                        pltpu.VMEM((tt, w), F32),
                        pltpu.VMEM((8, w), F32)],
        compiler_params=_params(("arbitrary", "arbitrary")),
        name="rg_lru",
    )(x_lru, g_lru, conv_w, conv_b.reshape(1, w), wg, bg, lam.reshape(1, w))


def _moba_top3_bias(gate, off, qi, lane):
    lanef = lane.astype(F32)
    past = (lane >= off) & (lane < off + qi)
    g = jnp.where(past, gate, -jnp.inf)
    sel = jnp.zeros(gate.shape, jnp.bool_)
    for _ in range(MOBA_TOPK):
        mx = jnp.max(g, axis=1, keepdims=True)
        first = jnp.min(jnp.where(g == mx, lanef, 1e9), axis=1, keepdims=True)
        pick = lanef == first
        sel = sel | pick
        g = jnp.where(pick, -jnp.inf, g)
    attend = (sel & past) | (lane == off + qi)
    block_lane = (lane >= off) & (lane < off + HEAD_DIM)
    return jnp.where(block_lane & jnp.logical_not(attend), MASK_BIAS, 0.0)


def _moba_kernel(nblk, q_ref, k_ref, v_ref, g_ref, o_ref,
                 ka, kb, va, vb, kma, kmb, cbias):
    qi = pl.program_id(2)
    bq = ATT_BLOCK
    lane = lax.broadcasted_iota(jnp.int32, (bq, LANES), 1)
    head_a = lane < HEAD_DIM

    @pl.when(qi == 0)
    def _prep():
        t = nblk * bq
        prow = lax.broadcasted_iota(jnp.int32, (LANES, t), 0)
        pcol = lax.broadcasted_iota(jnp.int32, (LANES, t), 1) // bq
        kall = k_ref[0]
        pool_a = jnp.where(prow - HEAD_DIM == pcol, 1.0 / bq, 0.0).astype(BF16)
        pool_b = jnp.where(prow == pcol, 1.0 / bq, 0.0).astype(BF16)
        lane_m = lax.broadcasted_iota(jnp.int32, (LANES, LANES), 1)
        kma[...] = jnp.where(lane_m < HEAD_DIM,
                             jnp.dot(pool_a, kall, preferred_element_type=F32), 0.0)
        kmb[...] = jnp.where(lane_m >= HEAD_DIM,
                             jnp.dot(pool_b, kall, preferred_element_type=F32), 0.0)
        ri = lax.broadcasted_iota(jnp.int32, (bq, bq), 0)
        ci = lax.broadcasted_iota(jnp.int32, (bq, bq), 1)
        cbias[...] = jnp.where(ci <= ri, 0.0, MASK_BIAS)

        def aug(n, carry):
            rows = pl.ds(pl.multiple_of(n * bq, bq), bq)
            kf = k_ref[0, rows, :].astype(F32)
            vf = v_ref[0, rows, :].astype(F32)
            ka[rows, :] = jnp.where(head_a, kf, jnp.where(lane == HEAD_DIM + n, 1.0, 0.0)).astype(BF16)
            kb[rows, :] = jnp.where(head_a, jnp.where(lane == n, 1.0, 0.0), kf).astype(BF16)
            va[rows, :] = jnp.where(head_a, vf, jnp.where(lane == HEAD_DIM, 1.0, 0.0)).astype(BF16)
            vb[rows, :] = jnp.where(head_a, jnp.where(lane == 0, 1.0, 0.0), vf).astype(BF16)
            return carry

        lax.fori_loop(0, nblk, aug, 0)

    qf = q_ref[0].astype(F32)
    q_a = jnp.where(head_a, qf, 0.0).astype(BF16)
    q_b = jnp.where(head_a, 0.0, qf).astype(BF16)
    gate_a = lax.dot_general(q_a, kma[...].astype(BF16), _NT, preferred_element_type=F32)
    gate_b = lax.dot_general(q_b, kmb[...].astype(BF16), _NT, preferred_element_type=F32)
    qa_aug = jnp.where(head_a, qf, _moba_top3_bias(gate_a, HEAD_DIM, qi, lane)).astype(BF16)
    qb_aug = jnp.where(head_a, _moba_top3_bias(gate_b, 0, qi, lane), qf).astype(BF16)

    def scores(q_aug, k_aug, j):
        rows = pl.ds(pl.multiple_of(j * bq, bq), bq)
        return lax.dot_general(q_aug, k_aug[rows, :], _NT, preferred_element_type=F32)

    def pv(p, v_aug, j):
        rows = pl.ds(pl.multiple_of(j * bq, bq), bq)
        return jnp.dot(p.astype(BF16), v_aug[rows, :], preferred_element_type=F32)

    def first(q_aug, k_aug, v_aug):
        s = scores(q_aug, k_aug, qi) + cbias[...]
        m = jnp.max(s, axis=1, keepdims=True)
        return m, pv(jnp.exp(s - m), v_aug, qi)

    def step(q_aug, k_aug, v_aug, j, m, acc):
        s = scores(q_aug, k_aug, j)
        m_new = jnp.maximum(m, jnp.max(s, axis=1, keepdims=True))
        acc = jnp.exp(m - m_new) * acc + pv(jnp.exp(s - m_new), v_aug, j)
        return m_new, acc

    def body(j, carry):
        m_a, acc_a, m_b, acc_b = carry
        m_a, acc_a = step(qa_aug, ka, va, j, m_a, acc_a)
        m_b, acc_b = step(qb_aug, kb, vb, j, m_b, acc_b)
        return m_a, acc_a, m_b, acc_b

    m_a, acc_a = first(qa_aug, ka, va)
    m_b, acc_b = first(qb_aug, kb, vb)
    _, acc_a, _, acc_b = lax.fori_loop(0, qi, body, (m_a, acc_a, m_b, acc_b))

    out = jnp.where(head_a, acc_a / acc_a[:, HEAD_DIM:HEAD_DIM + 1], acc_b / acc_b[:, 0:1])
    o_ref[0] = (out * _silu(g_ref[0])).astype(o_ref.dtype)


def _attention_call(kernel, q, k, v, g, extra_inputs, extra_specs, scratch, name):
    bsz, t, w = q.shape
    npair = w // LANES
    nblk = t // ATT_BLOCK
    qmap = lambda b, p, i: (b, i, p)
    kvmap = lambda b, p, i: (b, 0, p)
    return pl.pallas_call(
        functools.partial(kernel, nblk),
        grid=(bsz, npair, nblk),
        in_specs=[pl.BlockSpec((1, ATT_BLOCK, LANES), qmap),
                  pl.BlockSpec((1, t, LANES), kvmap),
                  pl.BlockSpec((1, t, LANES), kvmap),
                  pl.BlockSpec((1, ATT_BLOCK, LANES), qmap)] + extra_specs,
        out_specs=pl.BlockSpec((1, ATT_BLOCK, LANES), qmap),
        out_shape=jax.ShapeDtypeStruct((bsz, t, w), BF16),
        scratch_shapes=scratch,
        compiler_params=_params(("arbitrary", "arbitrary", "arbitrary")),
        name=name,
    )(q, k, v, g, *extra_inputs)


def _moba(q, k, v, g):
    t = q.shape[1]
    scratch = [pltpu.VMEM((t, LANES), BF16) for _ in range(4)]
    scratch += [pltpu.VMEM((LANES, LANES), F32), pltpu.VMEM((LANES, LANES), F32),
                pltpu.VMEM((ATT_BLOCK, ATT_BLOCK), F32)]
    return _attention_call(_moba_kernel, q, k, v, g, [], [], scratch, "moba")


def _sb_kernel(nblk, q_ref, k_ref, v_ref, g_ref, tri_ref, o_ref, carry_s, acc_s):
    qi = pl.program_id(2)
    bq = ATT_BLOCK
    lane = lax.broadcasted_iota(jnp.int32, (bq, LANES), 1)
    head_a = lane < HEAD_DIM
    ri = lax.broadcasted_iota(jnp.int32, (bq, bq), 0)
    ci = lax.broadcasted_iota(jnp.int32, (bq, bq), 1)
    strict = ci < ri
    qf = q_ref[0].astype(F32)

    def tile(q_e, j, diag):
        rows = pl.ds(pl.multiple_of(j * bq, bq), bq)
        z = lax.dot_general(q_e, k_ref[0, rows, :], _NT, preferred_element_type=F32)
        sp = _softplus(z)
        log_keep = -sp
        log_beta = z - sp
        if diag:
            log_keep = jnp.where(strict, log_keep, 0.0)
        hi = log_keep.astype(BF16)
        lo = (log_keep - hi.astype(F32)).astype(BF16)
        later = (jnp.dot(hi, tri_ref[...], preferred_element_type=F32)
                 + jnp.dot(lo, tri_ref[...], preferred_element_type=F32)
                 + carry_s[...])
        wgt = jnp.exp(log_beta + later)
        if diag:
            wgt = jnp.where(strict, wgt, 0.0)
        acc_s[...] += jnp.dot(wgt.astype(BF16), v_ref[0, rows, :], preferred_element_type=F32)
        carry = carry_s[...] + jnp.sum(log_keep, axis=1, keepdims=True)
        carry_s[...] = carry
        return jnp.max(carry) > SB_DEAD

    def head(q_e):
        carry_s[...] = jnp.zeros(carry_s.shape, F32)
        acc_s[...] = jnp.zeros(acc_s.shape, F32)
        alive = tile(q_e, qi, True)

        def cond(st):
            j, alive = st
            return jnp.logical_and(j >= 0, alive)

        def body(st):
            j, _ = st
            return j - 1, tile(q_e, j, False)

        lax.while_loop(cond, body, (qi - 1, alive))
        return acc_s[...]

    out_a = head(jnp.where(head_a, qf, 0.0).astype(BF16))
    out_b = head(jnp.where(head_a, 0.0, qf).astype(BF16))
    out = jnp.where(head_a, out_a, out_b)
    o_ref[0] = (out * _silu(g_ref[0])).astype(o_ref.dtype)


def _stick_breaking(q, k, v, g):
    ri = lax.broadcasted_iota(jnp.int32, (ATT_BLOCK, ATT_BLOCK), 0)
    ci = lax.broadcasted_iota(jnp.int32, (ATT_BLOCK, ATT_BLOCK), 1)
    tri = (ri > ci).astype(BF16)
    scratch = [pltpu.VMEM((ATT_BLOCK, 1), F32), pltpu.VMEM((ATT_BLOCK, LANES), F32)]
    spec = pl.BlockSpec((ATT_BLOCK, ATT_BLOCK), lambda b, p, i: (0, 0))
    return _attention_call(_sb_kernel, q, k, v, g, [tri], [spec], scratch, "stick_breaking")


def _s5_kernel(nchunk, dists, u_ref, wi_ref, win_ref, wout_ref, pq_ref, d_ref, o_ref):
    u = u_ref[0]
    ub = u.astype(BF16)
    y = jnp.dot(ub, wi_ref[0], preferred_element_type=F32)
    e = jnp.dot(ub, win_ref[0], preferred_element_type=F32)
    rows = e.shape[0]
    cidx = lax.broadcasted_iota(jnp.int32, (rows, LANES), 0) % nchunk
    pq = pq_ref[0]
    for li, dist in enumerate(dists):
        prev = pltpu.roll(e, dist, axis=0)
        swapped = pltpu.roll(prev, S5_STATE, axis=1)
        term = pq[2 * li:2 * li + 1] * prev + pq[2 * li + 1:2 * li + 2] * swapped
        e = e + jnp.where(cidx >= dist, term, 0.0)
    start = jnp.where(cidx >= 1, pltpu.roll(e, 1, axis=0), 0.0)
    y = y + jnp.dot(start.astype(BF16), wout_ref[0], preferred_element_type=F32)
    o_ref[0] = y + d_ref[0] * u


def _s5_matrices(lam_re, lam_im, log_step, b_re, b_im, c_re, c_im, d, nchunk):
    hp = lax.Precision.HIGHEST
    step = jnp.exp(log_step)[:, None]
    decay = jnp.exp(lam_re * step)
    ab_re = decay * jnp.cos(lam_im * step)
    ab_im = decay * jnp.sin(lam_im * step)
    den = lam_re * lam_re + lam_im * lam_im
    f_re = ((ab_re - 1.0) * lam_re + ab_im * lam_im) / den
    f_im = (ab_im * lam_re - (ab_re - 1.0) * lam_im) / den
    bb_re = f_re[..., None] * b_re - f_im[..., None] * b_im
    bb_im = f_re[..., None] * b_im + f_im[..., None] * b_re

    def power(n):
        n = jnp.asarray(n, F32)[..., None, None]
        mag = jnp.exp(n * (lam_re * step))
        ang = n * (lam_im * step)
        return mag * jnp.cos(ang), mag * jnp.sin(ang)

    c = S5_CHUNK
    g = lam_re.shape[0]
    steps = jnp.arange(c)
    pr, pi = power(steps)
    cb_re = (jnp.einsum("ghp,tgp,gpk->tghk", c_re, pr, bb_re, precision=hp)
             - jnp.einsum("ghp,tgp,gpk->tghk", c_re, pi, bb_im, precision=hp)
             - jnp.einsum("ghp,tgp,gpk->tghk", c_im, pr, bb_im, precision=hp)
             - jnp.einsum("ghp,tgp,gpk->tghk", c_im, pi, bb_re, precision=hp))
    lag = steps[None, :] - steps[:, None]
    kern = cb_re[jnp.clip(lag, 0, c - 1)]
    kern = jnp.where((lag >= 0)[:, :, None, None, None], kern, 0.0)
    w_intra = kern.transpose(2, 0, 4, 1, 3).reshape(g, c * S5_GROUP, c * S5_GROUP)
    qr, qi = power(c - 1 - steps)
    in_re = qr[..., None] * bb_re[None] - qi[..., None] * bb_im[None]
    in_im = qr[..., None] * bb_im[None] + qi[..., None] * bb_re[None]
    w_in = jnp.concatenate([in_re, in_im], axis=2)
    w_in = w_in.transpose(1, 0, 3, 2).reshape(g, c * S5_GROUP, 2 * S5_STATE)
    or_, oi = power(steps + 1)
    out_re = c_re[None] * or_[:, :, None, :] - c_im[None] * oi[:, :, None, :]
    out_im = -(c_re[None] * oi[:, :, None, :] + c_im[None] * or_[:, :, None, :])
    w_out = jnp.concatenate([out_re, out_im], axis=3)
    w_out = w_out.transpose(1, 3, 0, 2).reshape(g, 2 * S5_STATE, c * S5_GROUP)
    dists = [1 << i for i in range(max(nchunk - 1, 0).bit_length())]
    rows = []
    for dist in dists:
        ar, ai = power(jnp.asarray(c * dist))
        rows += [jnp.concatenate([ar, ar], axis=-1), jnp.concatenate([-ai, ai], axis=-1)]
    pq = jnp.stack(rows, axis=1) if rows else jnp.zeros((g, 2, 2 * S5_STATE), F32)
    d_row = jnp.tile(d, (1, c)).reshape(g, 1, c * S5_GROUP)
    return w_intra.astype(BF16), w_in.astype(BF16), w_out.astype(BF16), pq, d_row, dists


def _s5(u, lam_re, lam_im, log_step, b_re, b_im, c_re, c_im, d):
    bsz, t, w = u.shape
    g = w // S5_GROUP
    c = S5_CHUNK
    nchunk = t // c
    rows = bsz * nchunk
    cw = c * S5_GROUP
    w_intra, w_in, w_out, pq, d_row, dists = _s5_matrices(
        lam_re, lam_im, log_step, b_re, b_im, c_re, c_im, d, nchunk)
    ug = u.reshape(bsz, nchunk, c, g, S5_GROUP).transpose(3, 0, 1, 2, 4).reshape(g, rows, cw)
    grp = lambda i: (i, 0, 0)
    yg = pl.pallas_call(
        functools.partial(_s5_kernel, nchunk, tuple(dists)),
        grid=(g,),
        in_specs=[pl.BlockSpec((1, rows, cw), grp),
                  pl.BlockSpec((1, cw, cw), grp),
                  pl.BlockSpec((1, cw, 2 * S5_STATE), grp),
                  pl.BlockSpec((1, 2 * S5_STATE, cw), grp),
                  pl.BlockSpec((1,) + pq.shape[1:], grp),
                  pl.BlockSpec((1, 1, cw), grp)],
        out_specs=pl.BlockSpec((1, rows, cw), grp),
        out_shape=jax.ShapeDtypeStruct((g, rows, cw), F32),
        compiler_params=_params(("arbitrary",)),
        name="s5",
    )(ug, w_intra, w_in, w_out, pq, d_row)
    return yg.reshape(g, bsz, nchunk, c, S5_GROUP).transpose(1, 2, 3, 0, 4).reshape(bsz, t, w)


def _even_out_kernel(x_ref, mod_ref, ya_ref, yb_ref, w_ref, o_ref):
    half = ya_ref.shape[2]
    mix = (jnp.dot(ya_ref[0], w_ref[:half, :], preferred_element_type=F32)
           + jnp.dot(yb_ref[0], w_ref[half:, :], preferred_element_type=F32))
    o_ref[0] = x_ref[0] + mod_ref[0][2:3] * mix


def _odd_out_kernel(x_ref, mod_ref, ya_ref, y5_ref, g5_ref, gw_ref, gb_ref, w_ref, o_ref):
    half = ya_ref.shape[2]
    z = jnp.dot(y5_ref[0].astype(BF16), gw_ref[...], preferred_element_type=F32) + gb_ref[...]
    yb = z[:, :half] * _sigmoid(z[:, half:]) * _silu(g5_ref[0])
    mix = (jnp.dot(ya_ref[0], w_ref[:half, :], preferred_element_type=F32)
           + jnp.dot(yb.astype(BF16), w_ref[half:, :], preferred_element_type=F32))
    o_ref[0] = x_ref[0] + mod_ref[0][2:3] * mix


def _out_projection(kernel, x, mod, rows_in, consts, name, tm=512):
    bsz, t, d = x.shape
    row = lambda b, i: (b, i, 0)
    const = lambda b, i: (0, 0)
    in_specs = [pl.BlockSpec((1, tm, d), row), pl.BlockSpec((1, 3, d), lambda b, i: (b, 0, 0))]
    in_specs += [pl.BlockSpec((1, tm, a.shape[2]), row) for a in rows_in]
    in_specs += [pl.BlockSpec(a.shape, const) for a in consts]
    return pl.pallas_call(
        kernel,
        grid=(bsz, t // tm),
        in_specs=in_specs,
        out_specs=pl.BlockSpec((1, tm, d), row),
        out_shape=jax.ShapeDtypeStruct((bsz, t, d), F32),
        compiler_params=_params(("arbitrary", "arbitrary")),
        name=name,
    )(x, mod, *rows_in, *consts)


def _even_layer(x, c, norm_g, ada_w, ada_b, w_in, conv_w, conv_b, rg_w, rg_b, ig_w, ig_b,
                lam, q_g, k_g, w_out):
    nh = SEG // HEAD_DIM
    mod = _modulation(c, ada_w, ada_b)
    q_gain = (jnp.tile(q_g, nh) * HEAD_DIM ** -0.5).reshape(1, SEG)
    k_gain = jnp.tile(k_g, nh).reshape(1, SEG)
    x_lru, g_lru, q, k, v, g_att = _in_projection(
        x, mod, norm_g, w_in, q_gain, k_gain,
        kinds=("x", "x", "q", "k", "x", "x"), dtypes=(F32, F32, BF16, BF16, BF16, F32))
    y_lru = _rg_lru(x_lru, g_lru, conv_w, conv_b, rg_w, rg_b, ig_w, ig_b, lam)
    y_att = _moba(q, k, v, g_att)
    return _out_projection(_even_out_kernel, x, mod, [y_lru, y_att], [w_out.astype(BF16)],
                           "even_out")


def _odd_layer(x, c, norm_g, ada_w, ada_b, w_in, q_g, k_g, lam_re, lam_im, log_step,
               b_re, b_im, c_re, c_im, d, glu_w, glu_b, w_out):
    nh = SEG // HEAD_DIM
    mod = _modulation(c, ada_w, ada_b)
    q_gain = (jnp.tile(q_g, nh) * HEAD_DIM ** -0.5).reshape(1, SEG)
    k_gain = jnp.tile(k_g, nh).reshape(1, SEG)
    q, k, v, g_sb, u, g_s5 = _in_projection(
        x, mod, norm_g, w_in, q_gain, k_gain,
        kinds=("q", "k", "x", "x", "x", "x"), dtypes=(BF16, BF16, BF16, F32, F32, F32))
    y_sb = _stick_breaking(q, k, v, g_sb)
    y5 = _s5(u, lam_re, lam_im, log_step, b_re, b_im, c_re, c_im, d)
    return _out_projection(_odd_out_kernel, x, mod, [y_sb, y5, g_s5],
                           [glu_w.astype(BF16), glu_b.reshape(1, -1), w_out.astype(BF16)],
                           "odd_out")


def kernel(x, c, ev_norm, ev_ada_w, ev_ada_b, ev_w_in, ev_conv_w, ev_conv_b, ev_rgate_w,
           ev_rgate_b, ev_igate_w, ev_igate_b, ev_lru_lambda, ev_q_norm, ev_k_norm, ev_w_out,
           od_norm, od_ada_w, od_ada_b, od_w_in, od_q_norm, od_k_norm, od_s5_lambda_re,
           od_s5_lambda_im, od_s5_log_step, od_s5_b_re, od_s5_b_im, od_s5_c_re, od_s5_c_im,
           od_s5_d, od_glu_w, od_glu_b, od_w_out):
    depth = ev_norm.shape[0] + od_norm.shape[0]
    for layer in range(depth):
        i = layer // 2
        if layer % 2 == 0:
            x = _even_layer(x, c, ev_norm[i], ev_ada_w[i], ev_ada_b[i], ev_w_in[i], ev_conv_w[i],
                            ev_conv_b[i], ev_rgate_w[i], ev_rgate_b[i], ev_igate_w[i],
                            ev_igate_b[i], ev_lru_lambda[i], ev_q_norm[i], ev_k_norm[i],
                            ev_w_out[i])
        else:
            x = _odd_layer(x, c, od_norm[i], od_ada_w[i], od_ada_b[i], od_w_in[i], od_q_norm[i],
                           od_k_norm[i], od_s5_lambda_re[i], od_s5_lambda_im[i],
                           od_s5_log_step[i], od_s5_b_re[i], od_s5_b_im[i], od_s5_c_re[i],
                           od_s5_c_im[i], od_s5_d[i], od_glu_w[i], od_glu_b[i], od_w_out[i])
    return x
```

```python
import functools

import jax
import jax.numpy as jnp
from jax import lax
from jax.experimental import pallas as pl
from jax.experimental.pallas import tpu as pltpu

F32 = jnp.float32
BF16 = jnp.bfloat16

HEAD_DIM = 64
NORM_EPS = 1e-6
LRU_C = 8.0
CONV_WIDTH = 4
MOBA_BLOCK = 256
MOBA_TOPK = 3
S5_GROUP = 16
S5_CHUNK = 16
SEG = 512
LANES = 128
ATT_BLOCK = 256
MOBA_Q = 512
MOBA_KV = 1024
MASK_BIAS = -(2.0 ** 100)
SB_DEAD = -104.0
LOG2_E = 1.4426950408889634
VMEM_LIMIT = 56 * 1024 * 1024

_NT = (((1,), (1,)), ((), ()))


def _sigmoid(x):
    return 1.0 / (1.0 + jnp.exp(-x))


def _silu(x):
    return x * _sigmoid(x)


def _softplus(x):
    return jnp.maximum(x, 0.0) + jnp.log1p(jnp.exp(-jnp.abs(x)))


def _params(sem):
    return pltpu.CompilerParams(dimension_semantics=sem, vmem_limit_bytes=VMEM_LIMIT)


def _mod_kernel(c_ref, w_ref, b_ref, o_ref):
    sc = _silu(c_ref[...])
    o_ref[...] = jnp.dot(sc, w_ref[...], preferred_element_type=F32,
                         precision=lax.Precision.HIGHEST) + b_ref[...]


def _modulation(c, w, b):
    bsz, d = c.shape
    n = w.shape[1]
    tn = 1024
    out = pl.pallas_call(
        _mod_kernel,
        grid=(n // tn,),
        in_specs=[pl.BlockSpec((bsz, d), lambda j: (0, 0)),
                  pl.BlockSpec((d, tn), lambda j: (0, j)),
                  pl.BlockSpec((1, tn), lambda j: (0, j))],
        out_specs=pl.BlockSpec((bsz, tn), lambda j: (0, j)),
        out_shape=jax.ShapeDtypeStruct((bsz, n), F32),
        compiler_params=_params(("arbitrary",)),
        name="modulation",
    )(c, w, b.reshape(1, n))
    return out.reshape(bsz, 3, d)


def _inproj_kernel(kinds, x_ref, mod_ref, ng_ref, w_ref, qg_ref, kg_ref, ones_ref, *out_refs):
    x = x_ref[0]
    ms = jnp.mean(x * x, axis=-1, keepdims=True)
    y = x * lax.rsqrt(ms + NORM_EPS) * ng_ref[...]
    m = mod_ref[0]
    h = (y * (1.0 + m[1:2]) + m[0:1]).astype(BF16)
    for s, kind in enumerate(kinds):
        p = jnp.dot(h, w_ref[:, s * SEG:(s + 1) * SEG], preferred_element_type=F32)
        if kind in ("q", "k"):
            ssq = jnp.dot((p * p).astype(BF16), ones_ref[...], preferred_element_type=F32)
            gain = qg_ref if kind == "q" else kg_ref
            p = p * lax.rsqrt(ssq * (1.0 / HEAD_DIM) + NORM_EPS) * gain[...]
        out_refs[s][0] = p.astype(out_refs[s].dtype)


def _in_projection(x, mod, norm_g, w_in, q_gain, k_gain, kinds, dtypes, tm=512):
    bsz, t, d = x.shape
    nseg = len(kinds)
    head_ones = jnp.kron(jnp.eye(SEG // HEAD_DIM, dtype=F32),
                         jnp.ones((HEAD_DIM, HEAD_DIM), F32)).astype(BF16)
    row = lambda b, i: (b, i, 0)
    const = lambda b, i: (0, 0)
    return pl.pallas_call(
        functools.partial(_inproj_kernel, kinds),
        grid=(bsz, t // tm),
        in_specs=[pl.BlockSpec((1, tm, d), row),
                  pl.BlockSpec((1, 3, d), lambda b, i: (b, 0, 0)),
                  pl.BlockSpec((1, d), const),
                  pl.BlockSpec((d, nseg * SEG), const),
                  pl.BlockSpec((1, SEG), const),
                  pl.BlockSpec((1, SEG), const),
                  pl.BlockSpec((SEG, SEG), const)],
        out_specs=[pl.BlockSpec((1, tm, SEG), row) for _ in kinds],
        out_shape=[jax.ShapeDtypeStruct((bsz, t, SEG), dt) for dt in dtypes],
        compiler_params=_params(("arbitrary", "arbitrary")),
        name="in_projection",
    )(x, mod, norm_g.reshape(1, d), w_in.astype(BF16), q_gain, k_gain, head_ones)


def _lru_kernel(x_ref, g_ref, cw_ref, cb_ref, wg_ref, bg_ref, lam_ref, o_ref,
                xbuf, a_s, b_s, hcar):
    tt = x_ref.shape[1]
    w = x_ref.shape[2]
    ng = tt // 8

    @pl.when(pl.program_id(1) == 0)
    def _():
        xbuf[0:8, :] = jnp.zeros((8, w), F32)
        hcar[...] = jnp.zeros((8, w), F32)

    x = x_ref[0]
    xbuf[8:8 + tt, :] = x
    cw = cw_ref[...]
    xc = cb_ref[...] + cw[CONV_WIDTH - 1:CONV_WIDTH] * x
    for j in range(CONV_WIDTH - 1):
        xc = xc + cw[j:j + 1] * xbuf[5 + j:5 + j + tt, :]
    xbuf[0:8, :] = x[tt - 8:tt, :]

    gates = jnp.dot(xc.astype(BF16), wg_ref[...], preferred_element_type=F32) + bg_ref[...]
    r = _sigmoid(gates[:, :w])
    i = _sigmoid(gates[:, w:])
    log_a = (-LRU_C) * r * _softplus(-lam_ref[...])
    a = jnp.exp(log_a)
    b = jnp.sqrt(-jnp.tanh(log_a) * (a * a + 1.0)) * (i * xc)

    sub = lax.broadcasted_iota(jnp.int32, (tt, w), 0) % 8
    for dist in (1, 2, 4):
        a_prev = pltpu.roll(a, dist, axis=0)
        b_prev = pltpu.roll(b, dist, axis=0)
        keep = sub >= dist
        b = jnp.where(keep, a * b_prev + b, b)
        a = jnp.where(keep, a * a_prev, a)
    a_s[...] = a
    b_s[...] = b

    def body(gi, hprev):
        rows = pl.ds(pl.multiple_of(gi * 8, 8), 8)
        h = a_s[rows, :] * hprev + b_s[rows, :]
        b_s[rows, :] = h
        return h[7:8, :]

    hcar[0:1, :] = lax.fori_loop(0, ng, body, hcar[0:1, :])
    gate = g_ref[0]
    o_ref[0] = (b_s[...] * _silu(gate)).astype(o_ref.dtype)


def _block_diag(blocks):
    n, k, _ = blocks.shape
    eye = jnp.eye(n, dtype=blocks.dtype)
    return jnp.einsum("gij,gh->gihj", blocks, eye).reshape(n * k, n * k)


def _rg_lru(x_lru, g_lru, conv_w, conv_b, rg_w, rg_b, ig_w, ig_b, lam, tt=512):
    bsz, t, w = x_lru.shape
    wg = jnp.concatenate([_block_diag(rg_w), _block_diag(ig_w)], axis=1).astype(BF16)
    bg = jnp.concatenate([rg_b.reshape(1, w), ig_b.reshape(1, w)], axis=1)
    row = lambda b, i: (b, i, 0)
    const = lambda b, i: (0, 0)
    return pl.pallas_call(
        _lru_kernel,
        grid=(bsz, t // tt),
        in_specs=[pl.BlockSpec((1, tt, w), row),
                  pl.BlockSpec((1, tt, w), row),
                  pl.BlockSpec((CONV_WIDTH, w), const),
                  pl.BlockSpec((1, w), const),
                  pl.BlockSpec((w, 2 * w), const),
                  pl.BlockSpec((1, 2 * w), const),
                  pl.BlockSpec((1, w), const)],
        out_specs=pl.BlockSpec((1, tt, w), row),
        out_shape=jax.ShapeDtypeStruct((bsz, t, w), BF16),
        scratch_shapes=[pltpu.VMEM((tt + 8, w), F32),
                        pltpu.VMEM((tt, w), F32),
                        pltpu.VMEM((tt, w), F32),
                        pltpu.VMEM((8, w), F32)],
        compiler_params=_params(("arbitrary", "arbitrary")),
        name="rg_lru",
    )(x_lru, g_lru, conv_w, conv_b.reshape(1, w), wg, bg, lam.reshape(1, w))


def _moba_bias(gate, off, blk, tile_blk, lane):
    lanef = lane.astype(F32)
    past = (lane >= off) & (lane < off + blk)
    g = jnp.where(past, gate, -jnp.inf)
    sel = jnp.zeros(gate.shape, jnp.bool_)
    for _ in range(MOBA_TOPK):
        mx = jnp.max(g, axis=1, keepdims=True)
        first = jnp.min(jnp.where(g == mx, lanef, 1e9), axis=1, keepdims=True)
        pick = lanef == first
        sel = sel | pick
        g = jnp.where(pick, -jnp.inf, g)
    sel = sel & past
    block_lane = (lane >= off) & (lane < off + HEAD_DIM)
    before_tile = lane < off + tile_blk
    sweep = jnp.where(block_lane & jnp.logical_not(sel & before_tile), MASK_BIAS, 0.0)
    diag = jnp.where(block_lane & jnp.logical_not(sel | (lane == off + blk)), MASK_BIAS, 0.0)
    return sweep, diag


def _moba_kernel(nblk, q_ref, k_ref, v_ref, g_ref, o_ref,
                 ka, kb, va, vb, kma, kmb, dbias, m_a, acc_a, m_b, acc_b):
    qt = pl.program_id(2)
    bq = MOBA_Q
    blk_rows = MOBA_BLOCK
    lane = lax.broadcasted_iota(jnp.int32, (bq, LANES), 1)
    row = lax.broadcasted_iota(jnp.int32, (bq, LANES), 0)
    head_a = lane < HEAD_DIM
    tile_blk = (bq // blk_rows) * qt
    blk = tile_blk + row // blk_rows

    @pl.when(qt == 0)
    def _prep():
        t = nblk * blk_rows
        prow = lax.broadcasted_iota(jnp.int32, (LANES, t), 0)
        pcol = lax.broadcasted_iota(jnp.int32, (LANES, t), 1) // blk_rows
        kall = k_ref[0]
        pool_a = jnp.where(prow - HEAD_DIM == pcol, 1.0 / blk_rows, 0.0).astype(BF16)
        pool_b = jnp.where(prow == pcol, 1.0 / blk_rows, 0.0).astype(BF16)
        lane_m = lax.broadcasted_iota(jnp.int32, (LANES, LANES), 1)
        kma[...] = jnp.where(lane_m < HEAD_DIM,
                             jnp.dot(pool_a, kall, preferred_element_type=F32), 0.0)
        kmb[...] = jnp.where(lane_m >= HEAD_DIM,
                             jnp.dot(pool_b, kall, preferred_element_type=F32), 0.0)
        ri = lax.broadcasted_iota(jnp.int32, (bq, bq), 0)
        ci = lax.broadcasted_iota(jnp.int32, (bq, bq), 1)
        dbias[...] = jnp.where(ci <= ri, 0.0, MASK_BIAS)

        lane_b = lax.broadcasted_iota(jnp.int32, (blk_rows, LANES), 1)
        a_lanes = lane_b < HEAD_DIM

        def aug(n, carry):
            rows = pl.ds(pl.multiple_of(n * blk_rows, blk_rows), blk_rows)
            kf = k_ref[0, rows, :].astype(F32)
            vf = v_ref[0, rows, :].astype(F32)
            ka[rows, :] = jnp.where(a_lanes, kf, jnp.where(lane_b == HEAD_DIM + n, 1.0, 0.0)).astype(BF16)
            kb[rows, :] = jnp.where(a_lanes, jnp.where(lane_b == n, 1.0, 0.0), kf).astype(BF16)
            va[rows, :] = jnp.where(a_lanes, vf, jnp.where(lane_b == HEAD_DIM, 1.0, 0.0)).astype(BF16)
            vb[rows, :] = jnp.where(a_lanes, jnp.where(lane_b == 0, 1.0, 0.0), vf).astype(BF16)
            return carry

        lax.fori_loop(0, nblk, aug, 0)

    qf = q_ref[0].astype(F32)
    q_a = jnp.where(head_a, qf, 0.0).astype(BF16)
    q_b = jnp.where(head_a, 0.0, qf).astype(BF16)
    gate_a = lax.dot_general(q_a, kma[...].astype(BF16), _NT, preferred_element_type=F32)
    gate_b = lax.dot_general(q_b, kmb[...].astype(BF16), _NT, preferred_element_type=F32)
    sweep_a, diag_a = _moba_bias(gate_a, HEAD_DIM, blk, tile_blk, lane)
    sweep_b, diag_b = _moba_bias(gate_b, 0, blk, tile_blk, lane)
    qa_sweep = jnp.where(head_a, qf, sweep_a).astype(BF16)
    qa_diag = jnp.where(head_a, qf, diag_a).astype(BF16)
    qb_sweep = jnp.where(head_a, sweep_b, qf).astype(BF16)
    qb_diag = jnp.where(head_a, diag_b, qf).astype(BF16)

    def softmax_update(s, v_rows, m_s, acc_s, init):
        mx = jnp.max(s, axis=1, keepdims=True)
        if init:
            m_new = jnp.broadcast_to(mx, (bq, LANES))
        else:
            m = m_s[...]
            m_new = jnp.maximum(m, mx)
        m_s[...] = m_new
        p = jnp.exp2(s - jnp.concatenate([m_new] * (s.shape[1] // LANES), axis=1))
        pv = jnp.dot(p.astype(BF16), v_rows, preferred_element_type=F32)
        if init:
            acc_s[...] = pv
        else:
            acc_s[...] = jnp.exp2(m - m_new) * acc_s[...] + pv

    def first(q_aug, k_aug, v_aug, m_s, acc_s):
        rows = pl.ds(pl.multiple_of(qt * bq, bq), bq)
        s = lax.dot_general(q_aug, k_aug[rows, :], _NT, preferred_element_type=F32) + dbias[...]
        softmax_update(s, v_aug[rows, :], m_s, acc_s, True)

    first(qa_diag, ka, va, m_a, acc_a)
    first(qb_diag, kb, vb, m_b, acc_b)

    def body(j, carry):
        rows = pl.ds(pl.multiple_of(j * MOBA_KV, MOBA_KV), MOBA_KV)
        s_a = lax.dot_general(qa_sweep, ka[rows, :], _NT, preferred_element_type=F32)
        s_b = lax.dot_general(qb_sweep, kb[rows, :], _NT, preferred_element_type=F32)
        softmax_update(s_a, va[rows, :], m_a, acc_a, False)
        softmax_update(s_b, vb[rows, :], m_b, acc_b, False)
        return carry

    per_step = MOBA_KV // blk_rows
    lax.fori_loop(0, (tile_blk + per_step - 1) // per_step, body, 0)

    ra = acc_a[...]
    rb = acc_b[...]
    out = jnp.where(head_a, ra / ra[:, HEAD_DIM:HEAD_DIM + 1], rb / rb[:, 0:1])
    o_ref[0] = (out * _silu(g_ref[0])).astype(o_ref.dtype)


def _attention_call(kernel, bq, q, k, v, g, extra_inputs, extra_specs, scratch, name):
    bsz, t, w = q.shape
    npair = w // LANES
    qmap = lambda b, p, i: (b, i, p)
    kvmap = lambda b, p, i: (b, 0, p)
    return pl.pallas_call(
        functools.partial(kernel, t // MOBA_BLOCK),
        grid=(bsz, npair, t // bq),
        in_specs=[pl.BlockSpec((1, bq, LANES), qmap),
                  pl.BlockSpec((1, t, LANES), kvmap),
                  pl.BlockSpec((1, t, LANES), kvmap),
                  pl.BlockSpec((1, bq, LANES), qmap)] + extra_specs,
        out_specs=pl.BlockSpec((1, bq, LANES), qmap),
        out_shape=jax.ShapeDtypeStruct((bsz, t, w), BF16),
        scratch_shapes=scratch,
        compiler_params=_params(("arbitrary", "arbitrary", "arbitrary")),
        name=name,
    )(q, k, v, g, *extra_inputs)


def _moba(q, k, v, g):
    t = q.shape[1]
    assert t % MOBA_KV == 0 and t // MOBA_BLOCK <= HEAD_DIM
    scratch = [pltpu.VMEM((t, LANES), BF16) for _ in range(4)]
    scratch += [pltpu.VMEM((LANES, LANES), F32), pltpu.VMEM((LANES, LANES), F32),
                pltpu.VMEM((MOBA_Q, MOBA_Q), F32)]
    scratch += [pltpu.VMEM((MOBA_Q, LANES), F32), pltpu.VMEM((MOBA_Q, LANES), F32)] * 2
    return _attention_call(_moba_kernel, MOBA_Q, q, k, v, g, [], [], scratch, "moba")


def _sb_kernel(nblk, q_ref, k_ref, v_ref, g_ref, tri_ref, o_ref,
               carry_a, acc_a, carry_b, acc_b):
    qi = pl.program_id(2)
    bq = ATT_BLOCK
    lane = lax.broadcasted_iota(jnp.int32, (bq, LANES), 1)
    head_a = lane < HEAD_DIM
    ri = lax.broadcasted_iota(jnp.int32, (bq, bq), 0)
    ci = lax.broadcasted_iota(jnp.int32, (bq, bq), 1)
    strict = ci < ri
    qf = q_ref[0].astype(F32)

    def terms(q_e, j, mask):
        rows = pl.ds(pl.multiple_of(j * bq, bq), bq)
        z = lax.dot_general(q_e, k_ref[0, rows, :], _NT, preferred_element_type=F32)
        sp = _softplus(z)
        log_keep = -sp
        if mask is not None:
            log_keep = jnp.where(mask, log_keep, 0.0)
        hi = log_keep.astype(BF16)
        lo = (log_keep - hi.astype(F32)).astype(BF16)
        later = (jnp.dot(hi, tri_ref[...], preferred_element_type=F32)
                 + jnp.dot(lo, tri_ref[...], preferred_element_type=F32))
        return z - sp, later, jnp.sum(log_keep, axis=1, keepdims=True), rows

    def pv(wgt, rows):
        return jnp.dot(wgt.astype(BF16), v_ref[0, rows, :], preferred_element_type=F32)

    def start(q_e, carry_s, acc_s):
        has_prev = qi > 0
        lb_d, later_d, sum_d, rows_d = terms(q_e, qi, strict)
        lb_p, later_p, sum_p, rows_p = terms(q_e, jnp.maximum(qi - 1, 0), None)
        w_d = jnp.where(strict, jnp.exp(lb_d + later_d), 0.0)
        w_p = jnp.where(has_prev, jnp.exp(lb_p + later_p + sum_d), 0.0)
        acc_s[...] = pv(w_d, rows_d) + pv(w_p, rows_p)
        carry = sum_d + jnp.where(has_prev, sum_p, 0.0)
        carry_s[...] = carry
        return jnp.max(carry) > SB_DEAD

    def rest(q_e, carry_s, acc_s, alive):
        def cond(st):
            j, alive = st
            return jnp.logical_and(j >= 0, alive)

        def body(st):
            j, _ = st
            lb, later, total, rows = terms(q_e, j, None)
            carry = carry_s[...]
            acc_s[...] += pv(jnp.exp(lb + later + carry), rows)
            carry = carry + total
            carry_s[...] = carry
            return j - 1, jnp.max(carry) > SB_DEAD

        lax.while_loop(cond, body, (qi - 2, alive))

    q_a = jnp.where(head_a, qf, 0.0).astype(BF16)
    q_b = jnp.where(head_a, 0.0, qf).astype(BF16)
    alive_a = start(q_a, carry_a, acc_a)
    alive_b = start(q_b, carry_b, acc_b)
    rest(q_a, carry_a, acc_a, alive_a)
    rest(q_b, carry_b, acc_b, alive_b)
    out = jnp.where(head_a, acc_a[...], acc_b[...])
    o_ref[0] = (out * _silu(g_ref[0])).astype(o_ref.dtype)


def _stick_breaking(q, k, v, g):
    ri = lax.broadcasted_iota(jnp.int32, (ATT_BLOCK, ATT_BLOCK), 0)
    ci = lax.broadcasted_iota(jnp.int32, (ATT_BLOCK, ATT_BLOCK), 1)
    tri = (ri > ci).astype(BF16)
    scratch = [pltpu.VMEM((ATT_BLOCK, 1), F32), pltpu.VMEM((ATT_BLOCK, LANES), F32)] * 2
    spec = pl.BlockSpec((ATT_BLOCK, ATT_BLOCK), lambda b, p, i: (0, 0))
    return _attention_call(_sb_kernel, ATT_BLOCK, q, k, v, g, [tri], [spec], scratch,
                           "stick_breaking")


def _s5_kernel(u_ref, bw_ref, cw_ref, a_ref, a16_ref, d_ref, o_ref, st, carry):
    nl = d_ref.shape[1] // LANES
    n = u_ref.shape[1] // (S5_CHUNK * nl)
    half = st.shape[1] // 2
    sw = half // 2
    cin = d_ref.shape[1] // 2

    @pl.when(pl.program_id(1) == 0)
    def _():
        carry[...] = jnp.zeros(carry.shape, F32)

    def phase_rows(s, piece):
        return pl.ds(s * nl + piece, n, stride=S5_CHUNK * nl)

    def load_phase(s):
        return jnp.concatenate([u_ref[0, phase_rows(s, c), :] for c in range(nl)], axis=1)

    def cmul(coef, re, im, h):
        cr = coef[2 * h:2 * h + 1]
        ci = coef[2 * h + 1:2 * h + 2]
        return cr * re - ci * im, cr * im + ci * re

    def advance(u_bf, zero_state):
        a = a_ref[...]
        for h in range(2):
            bu = jnp.dot(u_bf[:, h * cin:(h + 1) * cin], bw_ref[h], preferred_element_type=F32)
            re_cols = slice(h * half, h * half + sw)
            im_cols = slice(h * half + sw, (h + 1) * half)
            if zero_state:
                st[:, re_cols] = bu[:, :sw]
                st[:, im_cols] = bu[:, sw:]
            else:
                re, im = cmul(a, st[:, re_cols], st[:, im_cols], h)
                st[:, re_cols] = re + bu[:, :sw]
                st[:, im_cols] = im + bu[:, sw:]

    for s in range(S5_CHUNK):
        advance(load_phase(s).astype(BF16), s == 0)

    a16 = a16_ref[...]

    def chain(c, prev):
        row = pl.ds(c, 1)
        local_end = st[row, :]
        st[row, :] = prev
        parts = []
        for h in range(2):
            re, im = cmul(a16, prev[:, h * half:h * half + sw],
                          prev[:, h * half + sw:(h + 1) * half], h)
            parts += [re, im]
        return jnp.concatenate(parts, axis=1) + local_end

    carry[0:1, :] = lax.fori_loop(0, n, chain, carry[0:1, :])

    for s in range(S5_CHUNK):
        u_s = load_phase(s)
        advance(u_s.astype(BF16), False)
        ys = [jnp.dot(st[:, h * half:(h + 1) * half].astype(BF16), cw_ref[h],
                      preferred_element_type=F32) for h in range(2)]
        y = jnp.concatenate(ys, axis=1) + d_ref[...] * u_s
        for c in range(nl):
            o_ref[0, phase_rows(s, c), :] = y[:, c * LANES:(c + 1) * LANES]


def _s5_matrices(lam_re, lam_im, log_step, b_re, b_im, c_re, c_im):
    g, p = lam_re.shape
    hg = g // 2
    step = jnp.exp(log_step)[:, None]
    decay = jnp.exp(lam_re * step)
    ab_re = decay * jnp.cos(lam_im * step)
    ab_im = decay * jnp.sin(lam_im * step)
    den = lam_re * lam_re + lam_im * lam_im
    f_re = ((ab_re - 1.0) * lam_re + ab_im * lam_im) / den
    f_im = (ab_im * lam_re - (ab_re - 1.0) * lam_im) / den
    bb_re = f_re[..., None] * b_re - f_im[..., None] * b_im
    bb_im = f_re[..., None] * b_im + f_im[..., None] * b_re
    n16 = float(S5_CHUNK)
    mag = jnp.exp(n16 * (lam_re * step))
    a16_re = mag * jnp.cos(n16 * (lam_im * step))
    a16_im = mag * jnp.sin(n16 * (lam_im * step))

    def coef_rows(re, im):
        re = re.reshape(2, hg * p)
        im = im.reshape(2, hg * p)
        return jnp.stack([re[0], im[0], re[1], im[1]], axis=0)

    eye = jnp.eye(hg, dtype=F32)

    def in_matrix(bb):
        bb = bb.reshape(2, hg, p, S5_GROUP)
        return jnp.einsum("xgpk,gj->xgkjp", bb, eye).reshape(2, hg * S5_GROUP, hg * p)

    def out_matrix(cc):
        cc = cc.reshape(2, hg, S5_GROUP, p)
        return jnp.einsum("xgkp,gj->xgpjk", cc, eye).reshape(2, hg * p, hg * S5_GROUP)

    bw = jnp.concatenate([in_matrix(bb_re), in_matrix(bb_im)], axis=2).astype(BF16)
    cw = jnp.concatenate([out_matrix(c_re), out_matrix(-c_im)], axis=1).astype(BF16)
    return bw, cw, coef_rows(ab_re, ab_im), coef_rows(a16_re, a16_im)


def _s5(u, lam_re, lam_im, log_step, b_re, b_im, c_re, c_im, d, tt=4096):
    bsz, t, w = u.shape
    tt = min(tt, t)
    g, p = lam_re.shape
    bw, cw, a_rows, a16_rows = _s5_matrices(lam_re, lam_im, log_step, b_re, b_im, c_re, c_im)
    ns = g * p * 2
    nl = w // LANES
    row = lambda b, i: (b, i, 0)
    c2 = lambda b, i: (0, 0)
    c3 = lambda b, i: (0, 0, 0)
    return pl.pallas_call(
        _s5_kernel,
        grid=(bsz, t // tt),
        in_specs=[pl.BlockSpec((1, tt * nl, LANES), row),
                  pl.BlockSpec(bw.shape, c3),
                  pl.BlockSpec(cw.shape, c3),
                  pl.BlockSpec(a_rows.shape, c2),
                  pl.BlockSpec(a16_rows.shape, c2),
                  pl.BlockSpec((1, w), c2)],
        out_specs=pl.BlockSpec((1, tt * nl, LANES), row),
        out_shape=jax.ShapeDtypeStruct((bsz, t * nl, LANES), F32),
        scratch_shapes=[pltpu.VMEM((tt // S5_CHUNK, ns), F32), pltpu.VMEM((8, ns), F32)],
        compiler_params=_params(("arbitrary", "arbitrary")),
        name="s5",
    )(u.reshape(bsz, t * nl, LANES), bw, cw, a_rows, a16_rows, d.reshape(1, w)).reshape(bsz, t, w)


def _even_out_kernel(x_ref, mod_ref, ya_ref, yb_ref, w_ref, o_ref):
    half = ya_ref.shape[2]
    mix = (jnp.dot(ya_ref[0], w_ref[:half, :], preferred_element_type=F32)
           + jnp.dot(yb_ref[0], w_ref[half:, :], preferred_element_type=F32))
    o_ref[0] = x_ref[0] + mod_ref[0][2:3] * mix


def _odd_out_kernel(x_ref, mod_ref, ya_ref, y5_ref, g5_ref, gw_ref, gb_ref, w_ref, o_ref):
    half = ya_ref.shape[2]
    z = jnp.dot(y5_ref[0].astype(BF16), gw_ref[...], preferred_element_type=F32) + gb_ref[...]
    yb = z[:, :half] * _sigmoid(z[:, half:]) * _silu(g5_ref[0])
    mix = (jnp.dot(ya_ref[0], w_ref[:half, :], preferred_element_type=F32)
           + jnp.dot(yb.astype(BF16), w_ref[half:, :], preferred_element_type=F32))
    o_ref[0] = x_ref[0] + mod_ref[0][2:3] * mix


def _out_projection(kernel, x, mod, rows_in, consts, name, tm=512):
    bsz, t, d = x.shape
    row = lambda b, i: (b, i, 0)
    const = lambda b, i: (0, 0)
    in_specs = [pl.BlockSpec((1, tm, d), row), pl.BlockSpec((1, 3, d), lambda b, i: (b, 0, 0))]
    in_specs += [pl.BlockSpec((1, tm, a.shape[2]), row) for a in rows_in]
    in_specs += [pl.BlockSpec(a.shape, const) for a in consts]
    return pl.pallas_call(
        kernel,
        grid=(bsz, t // tm),
        in_specs=in_specs,
        out_specs=pl.BlockSpec((1, tm, d), row),
        out_shape=jax.ShapeDtypeStruct((bsz, t, d), F32),
        compiler_params=_params(("arbitrary", "arbitrary")),
        name=name,
    )(x, mod, *rows_in, *consts)


def _even_layer(x, c, norm_g, ada_w, ada_b, w_in, conv_w, conv_b, rg_w, rg_b, ig_w, ig_b,
                lam, q_g, k_g, w_out):
    nh = SEG // HEAD_DIM
    mod = _modulation(c, ada_w, ada_b)
    q_gain = (jnp.tile(q_g, nh) * (HEAD_DIM ** -0.5 * LOG2_E)).reshape(1, SEG)
    k_gain = jnp.tile(k_g, nh).reshape(1, SEG)
    x_lru, g_lru, q, k, v, g_att = _in_projection(
        x, mod, norm_g, w_in, q_gain, k_gain,
        kinds=("x", "x", "q", "k", "x", "x"), dtypes=(F32, F32, BF16, BF16, BF16, F32))
    y_lru = _rg_lru(x_lru, g_lru, conv_w, conv_b, rg_w, rg_b, ig_w, ig_b, lam)
    y_att = _moba(q, k, v, g_att)
    return _out_projection(_even_out_kernel, x, mod, [y_lru, y_att], [w_out.astype(BF16)],
                           "even_out")


def _odd_layer(x, c, norm_g, ada_w, ada_b, w_in, q_g, k_g, lam_re, lam_im, log_step,
               b_re, b_im, c_re, c_im, d, glu_w, glu_b, w_out):
    nh = SEG // HEAD_DIM
    mod = _modulation(c, ada_w, ada_b)
    q_gain = (jnp.tile(q_g, nh) * HEAD_DIM ** -0.5).reshape(1, SEG)
    k_gain = jnp.tile(k_g, nh).reshape(1, SEG)
    q, k, v, g_sb, u, g_s5 = _in_projection(
        x, mod, norm_g, w_in, q_gain, k_gain,
        kinds=("q", "k", "x", "x", "x", "x"), dtypes=(BF16, BF16, BF16, F32, F32, F32))
    y_sb = _stick_breaking(q, k, v, g_sb)
    y5 = _s5(u, lam_re, lam_im, log_step, b_re, b_im, c_re, c_im, d)
    return _out_projection(_odd_out_kernel, x, mod, [y_sb, y5, g_s5],
                           [glu_w.astype(BF16), glu_b.reshape(1, -1), w_out.astype(BF16)],
                           "odd_out")


def kernel(x, c, ev_norm, ev_ada_w, ev_ada_b, ev_w_in, ev_conv_w, ev_conv_b, ev_rgate_w,
           ev_rgate_b, ev_igate_w, ev_igate_b, ev_lru_lambda, ev_q_norm, ev_k_norm, ev_w_out,
           od_norm, od_ada_w, od_ada_b, od_w_in, od_q_norm, od_k_norm, od_s5_lambda_re,
           od_s5_lambda_im, od_s5_log_step, od_s5_b_re, od_s5_b_im, od_s5_c_re, od_s5_c_im,
           od_s5_d, od_glu_w, od_glu_b, od_w_out):
    depth = ev_norm.shape[0] + od_norm.shape[0]
    for layer in range(depth):
        i = layer // 2
        if layer % 2 == 0:
            x = _even_layer(x, c, ev_norm[i], ev_ada_w[i], ev_ada_b[i], ev_w_in[i], ev_conv_w[i],
                            ev_conv_b[i], ev_rgate_w[i], ev_rgate_b[i], ev_igate_w[i],
                            ev_igate_b[i], ev_lru_lambda[i], ev_q_norm[i], ev_k_norm[i],
                            ev_w_out[i])
        else:
            x = _odd_layer(x, c, od_norm[i], od_ada_w[i], od_ada_b[i], od_w_in[i], od_q_norm[i],
                           od_k_norm[i], od_s5_lambda_re[i], od_s5_lambda_im[i],
                           od_s5_log_step[i], od_s5_b_re[i], od_s5_b_im[i], od_s5_c_re[i],
                           od_s5_c_im[i], od_s5_d[i], od_glu_w[i], od_glu_b[i], od_w_out[i])
    return x
```
